```python
import jax, jax.numpy as jnp
from jax import lax
import numpy as np

D_MODEL = 1024
BATCH = 16
SEQ = 4096
DEPTH = 2
DEC_BATCH = 32
DEC_SEQ = 2048
PAST_LEN = 128

D_A = D_MODEL
D_B = D_MODEL
D_C = D_MODEL
CHUNK = 128
SGU_HEADS = 8
SGU_HEAD_DIM = D_A // SGU_HEADS
SHORT_CONV_W = 3
CONF_CONV_W = 31
N_BRANCH = 3
D_FF = 4 * D_MODEL
IN_WIDTH = 2 * D_A + 3 * D_B + 2 * D_C + N_BRANCH * D_MODEL
ALPHA = float((2 * DEPTH) ** 0.25)
BETA = float((8 * DEPTH) ** -0.25)
LN_EPS = 1e-5

kernel_name = "hybrid_sgu_shortconv_conformer_encoder"


def _ln(x, g, b):
    xf = x.astype(jnp.float32)
    mu = jnp.mean(xf, axis=-1, keepdims=True)
    var = jnp.mean(jnp.square(xf - mu), axis=-1, keepdims=True)
    return ((xf - mu) * lax.rsqrt(var + LN_EPS) * g + b).astype(x.dtype)


def _dwconv(x, w):
    k = w.shape[0]
    return lax.conv_general_dilated(
        x, w[:, None, :].astype(x.dtype), window_strides=(1,), padding=[(k // 2, k // 2)],
        dimension_numbers=("NWC", "WIO", "NWC"), feature_group_count=x.shape[-1])


def _spatial_gating(u, v, g, b, w_s, b_s):
    bsz, s, _ = v.shape
    v = _ln(v, g, b).reshape(bsz, s // CHUNK, CHUNK, SGU_HEADS, SGU_HEAD_DIM)
    v = jnp.einsum("hpq,bcqhd->bcphd", w_s, v) + jnp.transpose(b_s)[:, :, None]
    return u * v.reshape(bsz, s, D_A)


def _mixer(x, w_in, b_in, sgu_ln_g, sgu_ln_b, sgu_w, sgu_b, sconv_w, cconv_w, cconv_b,
           cnorm_g, cnorm_b, w_branch, w_out, b_out):
    proj = jnp.einsum("bsd,de->bse", x, w_in) + b_in
    sizes = [D_A, D_A, D_B, D_B, D_B, D_C, D_C, D_MODEL, D_MODEL]
    idx = [int(i) for i in np.cumsum(sizes)]
    a_u, a_v, bg, cg, h, glu_a, glu_b, gate_a, gate_b, gate_c = jnp.split(proj, idx, axis=-1)
    y_a = _spatial_gating(jax.nn.gelu(a_u, approximate=False), jax.nn.gelu(a_v, approximate=False),
                          sgu_ln_g, sgu_ln_b, sgu_w, sgu_b)
    y_b = bg * _dwconv(cg * h, sconv_w)
    c = glu_a * jax.nn.sigmoid(glu_b)
    c = _dwconv(c, cconv_w) + cconv_b
    y_c = jax.nn.silu(_ln(c, cnorm_g, cnorm_b))
    merged = (jax.nn.sigmoid(gate_a) * jnp.einsum("bsc,cd->bsd", y_a, w_branch[0])
              + jax.nn.sigmoid(gate_b) * jnp.einsum("bsc,cd->bsd", y_b, w_branch[1])
              + jax.nn.sigmoid(gate_c) * jnp.einsum("bsc,cd->bsd", y_c, w_branch[2]))
    return jnp.einsum("bsd,de->bse", merged, w_out) + b_out


def _layer(x, w_in, b_in, sgu_ln_g, sgu_ln_b, sgu_w, sgu_b, sconv_w, cconv_w, cconv_b,
           cnorm_g, cnorm_b, w_branch, w_out, b_out, ln1_g, ln1_b,
           w_ff1, b_ff1, w_ff2, b_ff2, ln2_g, ln2_b):
    mix = _mixer(x, w_in, b_in, sgu_ln_g, sgu_ln_b, sgu_w, sgu_b, sconv_w, cconv_w, cconv_b,
                 cnorm_g, cnorm_b, w_branch, w_out, b_out)
    x = _ln(ALPHA * x + mix, ln1_g, ln1_b)
    hid = jnp.square(jax.nn.relu(jnp.einsum("bsd,df->bsf", x, w_ff1) + b_ff1))
    ff = jnp.einsum("bsf,fd->bsd", hid, w_ff2) + b_ff2
    return _ln(ALPHA * x + ff, ln2_g, ln2_b)


def _trunk(x, ln_in_g, ln_in_b, layer_params):
    x = _ln(x, ln_in_g, ln_in_b)
    for l in range(DEPTH):
        x = _layer(x, *[p[l] for p in layer_params])
    return x


def setup_inputs(seed: int = 0) -> dict:
    key = jax.random.key(seed)
    ks = jax.random.split(key, 32)
    f32 = jnp.float32
    nrm = lambda k, shape, s: jax.random.normal(k, shape, f32) * s
    L = DEPTH
    return {
        "x_prompt": nrm(ks[0], (BATCH, SEQ, D_MODEL), 1.0),
        "x_sample": nrm(ks[1], (DEC_BATCH, DEC_SEQ, D_MODEL), 1.0),
        "ln_in_g": 1.0 + nrm(ks[2], (D_MODEL,), 0.05),
        "ln_in_b": nrm(ks[3], (D_MODEL,), 0.02),
        "w_in": nrm(ks[4], (L, D_MODEL, IN_WIDTH), D_MODEL ** -0.5),
        "b_in": nrm(ks[5], (L, IN_WIDTH), 0.02),
        "sgu_ln_g": 1.0 + nrm(ks[6], (L, D_A), 0.05),
        "sgu_ln_b": nrm(ks[7], (L, D_A), 0.02),
        "sgu_w": nrm(ks[8], (L, SGU_HEADS, CHUNK, CHUNK), CHUNK ** -0.5),
        "sgu_b": 1.0 + nrm(ks[9], (L, SGU_HEADS, CHUNK), 0.05),
        "sconv_w": nrm(ks[10], (L, SHORT_CONV_W, D_B), SHORT_CONV_W ** -0.5),
        "cconv_w": nrm(ks[11], (L, CONF_CONV_W, D_C), CONF_CONV_W ** -0.5),
        "cconv_b": nrm(ks[12], (L, D_C), 0.02),
        "cnorm_g": 1.0 + nrm(ks[13], (L, D_C), 0.05),
        "cnorm_b": nrm(ks[14], (L, D_C), 0.02),
        "w_branch": nrm(ks[15], (L, N_BRANCH, D_MODEL, D_MODEL), BETA * D_MODEL ** -0.5),
        "w_out": nrm(ks[16], (L, D_MODEL, D_MODEL), BETA * D_MODEL ** -0.5),
        "b_out": nrm(ks[17], (L, D_MODEL), 0.02),
        "ln1_g": 1.0 + nrm(ks[18], (L, D_MODEL), 0.05),
        "ln1_b": nrm(ks[19], (L, D_MODEL), 0.02),
        "w_ff1": nrm(ks[20], (L, D_MODEL, D_FF), D_MODEL ** -0.5),
        "b_ff1": nrm(ks[21], (L, D_FF), 0.02),
        "w_ff2": nrm(ks[22], (L, D_FF, D_MODEL), BETA * D_FF ** -0.5),
        "b_ff2": nrm(ks[23], (L, D_MODEL), 0.02),
        "ln2_g": 1.0 + nrm(ks[24], (L, D_MODEL), 0.05),
        "ln2_b": nrm(ks[25], (L, D_MODEL), 0.02),
    }


def reference(x_prompt, x_sample, ln_in_g, ln_in_b, w_in, b_in, sgu_ln_g, sgu_ln_b, sgu_w, sgu_b,
              sconv_w, cconv_w, cconv_b, cnorm_g, cnorm_b, w_branch, w_out, b_out,
              ln1_g, ln1_b, w_ff1, b_ff1, w_ff2, b_ff2, ln2_g, ln2_b):
    layer_params = (w_in, b_in, sgu_ln_g, sgu_ln_b, sgu_w, sgu_b, sconv_w, cconv_w, cconv_b,
                    cnorm_g, cnorm_b, w_branch, w_out, b_out, ln1_g, ln1_b,
                    w_ff1, b_ff1, w_ff2, b_ff2, ln2_g, ln2_b)
    y_prompt = _trunk(x_prompt, ln_in_g, ln_in_b, layer_params)
    y_sample = _trunk(x_sample, ln_in_g, ln_in_b, layer_params)
    return (y_prompt, y_sample)
```

```python
import functools

import jax
import jax.numpy as jnp
from jax.experimental import pallas as pl
from jax.experimental.pallas import tpu as pltpu

D = 1024
D_FF = 4 * D
DEPTH = 2
CHUNK = 128
HEADS = 8
HEAD_DIM = D // HEADS
SHORT_W = 3
CONF_W = 31
ALPHA = float((2 * DEPTH) ** 0.25)
LN_EPS = 1e-5

LANES = 128
SLABS = D // LANES
HALO = 16
ROW_STRIDE = 4
VMEM_LIMIT_BYTES = 58 * 1024 * 1024

C_U, C_V, C_BG, C_CG, C_H, C_GA, C_GB, C_GATE = (i * D for i in range(8))

BF16 = jnp.bfloat16
F32 = jnp.float32


def _ln(x, g, b):
    mu = jnp.mean(x, axis=-1, keepdims=True)
    xc = x - mu
    var = jnp.mean(xc * xc, axis=-1, keepdims=True)
    return xc * jax.lax.rsqrt(var + LN_EPS) * g + b


def _gelu(x):
    return 0.5 * x * (1.0 + jax.lax.erf(x * (2.0 ** -0.5)))


def _sigmoid(x):
    return 1.0 / (1.0 + jnp.exp(-x))


def _dot(a, b):
    return jnp.dot(a, b, preferred_element_type=F32)


def _mixer_body(tile, tiles_per_seq, apply_ln_in,
                xp_ref, xc_ref, xn_ref, lnin_g, lnin_b, w_in, b_in, sln_g, sln_b,
                sgu_w, sgu_bb, sconv_w, cconv_w, cconv_b, cn_g, cn_b,
                w_br, w_out, b_out, ln1_g, ln1_b,
                o_ref,
                xb_scr, xres_scr, v_scr, sg_scr, g_scr, bg_scr, c_scr, yb_scr, yc_scr):
    rows = tile + 2 * HALO
    nq = tile // ROW_STRIDE
    i = pl.program_id(0)
    pos = i % tiles_per_seq
    first = pos == 0
    last = pos == tiles_per_seq - 1

    def proj(x_bf16, col, width=D):
        return _dot(x_bf16, w_in[:, col:col + width]) + b_in[:, col:col + width]

    xp, xc, xn = xp_ref[...], xc_ref[...], xn_ref[...]
    if apply_ln_in:
        xp = _ln(xp, lnin_g[...], lnin_b[...])
        xc = _ln(xc, lnin_g[...], lnin_b[...])
        xn = _ln(xn, lnin_g[...], lnin_b[...])
    xres_scr[...] = xc
    xb_scr[0:HALO, :] = xp.astype(BF16)
    xb_scr[HALO:HALO + tile, :] = xc.astype(BF16)
    xb_scr[HALO + tile:rows, :] = xn.astype(BF16)
    xb_c = xb_scr[HALO:HALO + tile, :]
    xb_all = xb_scr[...]

    def store_slabs(dst, val):
        for j in range(SLABS):
            lanes = slice(j * LANES, (j + 1) * LANES)
            dst[j, 0:HALO, :] = jnp.where(first, 0.0, val[0:HALO, lanes])
            dst[j, HALO:HALO + tile, :] = val[HALO:HALO + tile, lanes]
            dst[j, HALO + tile:rows, :] = jnp.where(last, 0.0, val[HALO + tile:rows, lanes])

    def gather_slabs(src):
        return jnp.concatenate([src[j] for j in range(SLABS)], axis=-1)

    v = _ln(_gelu(proj(xb_c, C_V)), sln_g[...], sln_b[...])
    v_scr[...] = v.astype(BF16)
    n_chunks = tile // CHUNK
    for h in range(HEADS):
        lanes = slice(h * HEAD_DIM, (h + 1) * HEAD_DIM)
        rhs = jnp.concatenate(
            [v_scr[c * CHUNK:(c + 1) * CHUNK, lanes] for c in range(n_chunks)], axis=-1)
        mixed = _dot(sgu_w[h], rhs)
        for c in range(n_chunks):
            sg_scr[c * CHUNK:(c + 1) * CHUNK, lanes] = (
                mixed[:, c * HEAD_DIM:(c + 1) * HEAD_DIM] + sgu_bb[h])
    y_a = _gelu(proj(xb_c, C_U)) * sg_scr[...]
    merged = _sigmoid(proj(xb_c, C_GATE)) * _dot(y_a.astype(BF16), w_br[0])

    store_slabs(g_scr, proj(xb_all, C_CG) * proj(xb_all, C_H))
    bg = proj(xb_c, C_BG)
    for j in range(SLABS):
        bg_scr[j] = bg[:, j * LANES:(j + 1) * LANES]
    for j in range(SLABS):
        lanes = slice(j * LANES, (j + 1) * LANES)
        for r in range(ROW_STRIDE):
            acc = None
            for k in range(SHORT_W):
                gk = g_scr[j, pl.ds(HALO + r + k - SHORT_W // 2, nq, stride=ROW_STRIDE), :]
                term = gk * sconv_w[k:k + 1, lanes]
                acc = term if acc is None else acc + term
            yb_scr[j, pl.ds(r, nq, stride=ROW_STRIDE), :] = (
                acc * bg_scr[j, pl.ds(r, nq, stride=ROW_STRIDE), :])
    y_b = gather_slabs(yb_scr)
    merged = merged + _sigmoid(proj(xb_c, C_GATE + D)) * _dot(y_b.astype(BF16), w_br[1])

    store_slabs(c_scr, proj(xb_all, C_GA) * _sigmoid(proj(xb_all, C_GB)))
    for j in range(SLABS):
        lanes = slice(j * LANES, (j + 1) * LANES)
        for r in range(ROW_STRIDE):
            acc = None
            for k in range(CONF_W):
                ck = c_scr[j, pl.ds(HALO + r + k - CONF_W // 2, nq, stride=ROW_STRIDE), :]
                term = ck * cconv_w[k:k + 1, lanes]
                acc = term if acc is None else acc + term
            yc_scr[j, pl.ds(r, nq, stride=ROW_STRIDE), :] = acc + cconv_b[:, lanes]
    cn = _ln(gather_slabs(yc_scr), cn_g[...], cn_b[...])
    y_c = cn * _sigmoid(cn)
    merged = merged + _sigmoid(proj(xb_c, C_GATE + 2 * D)) * _dot(y_c.astype(BF16), w_br[2])

    mix = _dot(merged.astype(BF16), w_out[...]) + b_out[...]
    o_ref[...] = _ln(ALPHA * xres_scr[...] + mix, ln1_g[...], ln1_b[...])


def _const_spec(shape):
    zeros = (0,) * len(shape)
    return pl.BlockSpec(shape, lambda i: zeros, pipeline_mode=pl.Buffered(1))


def _mixer(x, seq_len, tile, apply_ln_in, p):
    n = x.shape[0]
    rows = tile + 2 * HALO
    halo_blocks_per_tile = tile // HALO
    last_halo_block = n // HALO - 1
    body = functools.partial(_mixer_body, tile, seq_len // tile, apply_ln_in)
    row_spec = lambda: _const_spec((1, D))
    in_specs = [
        pl.BlockSpec((HALO, D), lambda i: (jnp.maximum(i * halo_blocks_per_tile - 1, 0), 0)),
        pl.BlockSpec((tile, D), lambda i: (i, 0)),
        pl.BlockSpec((HALO, D),
                     lambda i: (jnp.minimum((i + 1) * halo_blocks_per_tile, last_halo_block), 0)),
        row_spec(), row_spec(),
        _const_spec((D, 10 * D)), _const_spec((1, 10 * D)),
        row_spec(), row_spec(),
        _const_spec((HEADS, CHUNK, CHUNK)), _const_spec((HEADS, CHUNK, HEAD_DIM)),
        _const_spec((SHORT_W, D)), _const_spec((CONF_W, D)),
        row_spec(), row_spec(), row_spec(),
        _const_spec((3, D, D)), _const_spec((D, D)), row_spec(),
        row_spec(), row_spec(),
    ]
    scratch = [
        pltpu.VMEM((rows, D), BF16),
        pltpu.VMEM((tile, D), F32),
        pltpu.VMEM((tile, D), BF16),
        pltpu.VMEM((tile, D), F32),
        pltpu.VMEM((SLABS, rows, LANES), F32),
        pltpu.VMEM((SLABS, tile, LANES), F32),
        pltpu.VMEM((SLABS, rows, LANES), F32),
        pltpu.VMEM((SLABS, tile, LANES), F32),
        pltpu.VMEM((SLABS, tile, LANES), F32),
    ]
    return pl.pallas_call(
        body,
        grid=(n // tile,),
        in_specs=in_specs,
        out_specs=pl.BlockSpec((tile, D), lambda i: (i, 0)),
        out_shape=jax.ShapeDtypeStruct((n, D), F32),
        scratch_shapes=scratch,
        compiler_params=pltpu.CompilerParams(
            dimension_semantics=("arbitrary",), vmem_limit_bytes=VMEM_LIMIT_BYTES),
        name="mixer",
    )(x, x, x, *p)


def _ffn_body(x_ref, w1_ref, b1_ref, w2_ref, b2_ref, g_ref, bt_ref, o_ref):
    x = x_ref[...]
    h = jnp.maximum(_dot(x.astype(BF16), w1_ref[...]) + b1_ref[...], 0.0)
    ff = _dot((h * h).astype(BF16), w2_ref[...]) + b2_ref[...]
    o_ref[...] = _ln(ALPHA * x + ff, g_ref[...], bt_ref[...])


def _ffn(x, tile, p):
    n = x.shape[0]
    return pl.pallas_call(
        _ffn_body,
        grid=(n // tile,),
        in_specs=[pl.BlockSpec((tile, D), lambda i: (i, 0)),
                  _const_spec((D, D_FF)), _const_spec((1, D_FF)), _const_spec((D_FF, D)),
                  _const_spec((1, D)), _const_spec((1, D)), _const_spec((1, D))],
        out_specs=pl.BlockSpec((tile, D), lambda i: (i, 0)),
        out_shape=jax.ShapeDtypeStruct((n, D), F32),
        compiler_params=pltpu.CompilerParams(
            dimension_semantics=("arbitrary",), vmem_limit_bytes=VMEM_LIMIT_BYTES),
        name="ffn",
    )(x, *p)


MIXER_TILE = 256
FFN_TILE = 512


def kernel(x_prompt, x_sample, ln_in_g, ln_in_b, w_in, b_in, sgu_ln_g, sgu_ln_b, sgu_w, sgu_b, sconv_w, cconv_w, cconv_b, cnorm_g, cnorm_b, w_branch, w_out, b_out, ln1_g, ln1_b, w_ff1, b_ff1, w_ff2, b_ff2, ln2_g, ln2_b):
    row = lambda a: a.reshape(1, -1)
    mixer_params, ffn_params = [], []
    for l in range(DEPTH):
        mixer_params.append((
            row(ln_in_g), row(ln_in_b),
            w_in[l].astype(BF16), row(b_in[l]),
            row(sgu_ln_g[l]), row(sgu_ln_b[l]),
            sgu_w[l].astype(BF16),
            jnp.broadcast_to(sgu_b[l][:, :, None], (HEADS, CHUNK, HEAD_DIM)),
            sconv_w[l], cconv_w[l],
            row(cconv_b[l]), row(cnorm_g[l]), row(cnorm_b[l]),
            w_branch[l].astype(BF16), w_out[l].astype(BF16), row(b_out[l]),
            row(ln1_g[l]), row(ln1_b[l])))
        ffn_params.append((
            w_ff1[l].astype(BF16), row(b_ff1[l]), w_ff2[l].astype(BF16), row(b_ff2[l]),
            row(ln2_g[l]), row(ln2_b[l])))

    def trunk(x3d):
        seq_len = x3d.shape[1]
        x = x3d.reshape(-1, D)
        for l in range(DEPTH):
            x = _mixer(x, seq_len, MIXER_TILE, l == 0, mixer_params[l])
            x = _ffn(x, FFN_TILE, ffn_params[l])
        return x.reshape(x3d.shape)

    return trunk(x_prompt), trunk(x_sample)
```

```python
import functools

import jax
import jax.numpy as jnp
from jax.experimental import pallas as pl
from jax.experimental.pallas import tpu as pltpu

D = 1024
D_FF = 4 * D
DEPTH = 2
CHUNK = 128
HEADS = 8
HEAD_DIM = D // HEADS
SHORT_W = 3
CONF_W = 31
ALPHA = float((2 * DEPTH) ** 0.25)
LN_EPS = 1e-5

LANES = 128
SLABS = D // LANES
HALO = 16
ROW_STRIDE = 4
PIECE = 256
VMEM_LIMIT_BYTES = 58 * 1024 * 1024

C_U, C_V, C_BG, C_CG, C_H, C_GA, C_GB, C_GATE = (i * D for i in range(8))

BF16 = jnp.bfloat16
F32 = jnp.float32


def _ln(x, g, b):
    mu = jnp.mean(x, axis=-1, keepdims=True)
    xc = x - mu
    var = jnp.mean(xc * xc, axis=-1, keepdims=True)
    return xc * jax.lax.rsqrt(var + LN_EPS) * g + b


def _gelu(x):
    return 0.5 * x * (1.0 + jax.lax.erf(x * (2.0 ** -0.5)))


def _sigmoid(x):
    return 1.0 / (1.0 + jnp.exp(-x))


def _dot(a, b):
    return jnp.dot(a, b, preferred_element_type=F32)


def _mixer_body(tile, tiles_per_seq, apply_ln_in,
                xp_ref, xc_ref, xn_ref, lnin_g, lnin_b, w_in, b_in, sln_g, sln_b,
                sgu_w, sgu_bb, sconv_w, cconv_w, cconv_b, cn_g, cn_b,
                w_br, w_out, b_out, ln1_g, ln1_b,
                o_ref,
                xb_scr, xres_scr, v_scr, sg_scr, g_scr, bg_scr, c_scr, yb_scr, yc_scr,
                v_scr32, u_scr, gate_scr):
    rows = tile + 2 * HALO
    nq = tile // ROW_STRIDE
    i = pl.program_id(0)
    pos = i % tiles_per_seq
    first = pos == 0
    last = pos == tiles_per_seq - 1

    def proj(x_bf16, col, q):
        cols = slice(col + q * PIECE, col + (q + 1) * PIECE)
        return _dot(x_bf16, w_in[:, cols]) + b_in[:, cols]

    xp, xc, xn = xp_ref[...], xc_ref[...], xn_ref[...]
    if apply_ln_in:
        xp = _ln(xp, lnin_g[...], lnin_b[...])
        xc = _ln(xc, lnin_g[...], lnin_b[...])
        xn = _ln(xn, lnin_g[...], lnin_b[...])
    xres_scr[...] = xc
    xb_scr[0:HALO, :] = xp.astype(BF16)
    xb_scr[HALO:HALO + tile, :] = xc.astype(BF16)
    xb_scr[HALO + tile:rows, :] = xn.astype(BF16)
    xb_c = xb_scr[HALO:HALO + tile, :]
    xb_all = xb_scr[...]
    n_pieces = D // PIECE
    slabs_per_piece = PIECE // LANES

    def store_slabs(dst, q, val):
        for jj in range(slabs_per_piece):
            j = q * slabs_per_piece + jj
            lanes = slice(jj * LANES, (jj + 1) * LANES)
            dst[j, 0:HALO, :] = jnp.where(first, 0.0, val[0:HALO, lanes])
            dst[j, HALO:HALO + tile, :] = val[HALO:HALO + tile, lanes]
            dst[j, HALO + tile:rows, :] = jnp.where(last, 0.0, val[HALO + tile:rows, lanes])

    def gather_slabs(src):
        return jnp.concatenate([src[j] for j in range(SLABS)], axis=-1)

    def conv31_unit(j, r):
        lanes = slice(j * LANES, (j + 1) * LANES)
        acc = None
        for k in range(CONF_W):
            ck = c_scr[j, pl.ds(HALO + r + k - CONF_W // 2, nq, stride=ROW_STRIDE), :]
            term = ck * cconv_w[k:k + 1, lanes]
            acc = term if acc is None else acc + term
        yc_scr[j, pl.ds(r, nq, stride=ROW_STRIDE), :] = acc + cconv_b[:, lanes]

    def conv3_unit(j, r):
        lanes = slice(j * LANES, (j + 1) * LANES)
        acc = None
        for k in range(SHORT_W):
            gk = g_scr[j, pl.ds(HALO + r + k - SHORT_W // 2, nq, stride=ROW_STRIDE), :]
            term = gk * sconv_w[k:k + 1, lanes]
            acc = term if acc is None else acc + term
        yb_scr[j, pl.ds(r, nq, stride=ROW_STRIDE), :] = (
            acc * bg_scr[j, pl.ds(r, nq, stride=ROW_STRIDE), :])

    for q in range(n_pieces):
        store_slabs(c_scr, q, proj(xb_all, C_GA, q) * _sigmoid(proj(xb_all, C_GB, q)))

    def piece_cg_h(q):
        store_slabs(g_scr, q, proj(xb_all, C_CG, q) * proj(xb_all, C_H, q))

    def piece_bg(q):
        bg = proj(xb_c, C_BG, q)
        for jj in range(slabs_per_piece):
            bg_scr[q * slabs_per_piece + jj] = bg[:, jj * LANES:(jj + 1) * LANES]

    def piece_v(q):
        v_scr32[:, q * PIECE:(q + 1) * PIECE] = _gelu(proj(xb_c, C_V, q))

    def piece_u(q):
        u_scr[:, q * PIECE:(q + 1) * PIECE] = _gelu(proj(xb_c, C_U, q))

    def piece_gate(n, q):
        gate_scr[n, :, q * PIECE:(q + 1) * PIECE] = _sigmoid(proj(xb_c, C_GATE + n * D, q))

    pieces = ([functools.partial(piece_cg_h, q) for q in range(n_pieces)]
              + [functools.partial(piece_bg, q) for q in range(n_pieces)]
              + [functools.partial(piece_v, q) for q in range(n_pieces)]
              + [functools.partial(piece_u, q) for q in range(n_pieces)]
              + [functools.partial(piece_gate, n, q) for n in range(3) for q in range(n_pieces)])
    conv31_units = [(j, r) for j in range(SLABS) for r in range(ROW_STRIDE)]
    for unit in conv31_units:
        conv31_unit(*unit)
        if pieces:
            pieces.pop(0)()
    for piece in pieces:
        piece()

    v_scr[...] = _ln(v_scr32[...], sln_g[...], sln_b[...]).astype(BF16)
    n_chunks = tile // CHUNK
    for h in range(HEADS):
        lanes = slice(h * HEAD_DIM, (h + 1) * HEAD_DIM)
        rhs = jnp.concatenate(
            [v_scr[c * CHUNK:(c + 1) * CHUNK, lanes] for c in range(n_chunks)], axis=-1)
        mixed = _dot(sgu_w[h], rhs)
        for c in range(n_chunks):
            sg_scr[c * CHUNK:(c + 1) * CHUNK, lanes] = (
                mixed[:, c * HEAD_DIM:(c + 1) * HEAD_DIM] + sgu_bb[h])
        for r in range(ROW_STRIDE):
            conv3_unit(h, r)

    y_b = gather_slabs(yb_scr)
    merged = gate_scr[1] * _dot(y_b.astype(BF16), w_br[1])
    y_a = u_scr[...] * sg_scr[...]
    merged = merged + gate_scr[0] * _dot(y_a.astype(BF16), w_br[0])
    cn = _ln(gather_slabs(yc_scr), cn_g[...], cn_b[...])
    y_c = cn * _sigmoid(cn)
    merged = merged + gate_scr[2] * _dot(y_c.astype(BF16), w_br[2])

    mix = _dot(merged.astype(BF16), w_out[...]) + b_out[...]
    o_ref[...] = _ln(ALPHA * xres_scr[...] + mix, ln1_g[...], ln1_b[...])


def _const_spec(shape):
    zeros = (0,) * len(shape)
    return pl.BlockSpec(shape, lambda i: zeros, pipeline_mode=pl.Buffered(1))


def _mixer(x, seq_len, tile, apply_ln_in, p):
    n = x.shape[0]
    rows = tile + 2 * HALO
    halo_blocks_per_tile = tile // HALO
    last_halo_block = n // HALO - 1
    body = functools.partial(_mixer_body, tile, seq_len // tile, apply_ln_in)
    row_spec = lambda: _const_spec((1, D))
    in_specs = [
        pl.BlockSpec((HALO, D), lambda i: (jnp.maximum(i * halo_blocks_per_tile - 1, 0), 0)),
        pl.BlockSpec((tile, D), lambda i: (i, 0)),
        pl.BlockSpec((HALO, D),
                     lambda i: (jnp.minimum((i + 1) * halo_blocks_per_tile, last_halo_block), 0)),
        row_spec(), row_spec(),
        _const_spec((D, 10 * D)), _const_spec((1, 10 * D)),
        row_spec(), row_spec(),
        _const_spec((HEADS, CHUNK, CHUNK)), _const_spec((HEADS, CHUNK, HEAD_DIM)),
        _const_spec((SHORT_W, D)), _const_spec((CONF_W, D)),
        row_spec(), row_spec(), row_spec(),
        _const_spec((3, D, D)), _const_spec((D, D)), row_spec(),
        row_spec(), row_spec(),
    ]
    scratch = [
        pltpu.VMEM((rows, D), BF16),
        pltpu.VMEM((tile, D), F32),
        pltpu.VMEM((tile, D), BF16),
        pltpu.VMEM((tile, D), F32),
        pltpu.VMEM((SLABS, rows, LANES), F32),
        pltpu.VMEM((SLABS, tile, LANES), F32),
        pltpu.VMEM((SLABS, rows, LANES), F32),
        pltpu.VMEM((SLABS, tile, LANES), F32),
        pltpu.VMEM((SLABS, tile, LANES), F32),
        pltpu.VMEM((tile, D), F32),
        pltpu.VMEM((tile, D), F32),
        pltpu.VMEM((3, tile, D), F32),
    ]
    return pl.pallas_call(
        body,
        grid=(n // tile,),
        in_specs=in_specs,
        out_specs=pl.BlockSpec((tile, D), lambda i: (i, 0)),
        out_shape=jax.ShapeDtypeStruct((n, D), F32),
        scratch_shapes=scratch,
        compiler_params=pltpu.CompilerParams(
            dimension_semantics=("arbitrary",), vmem_limit_bytes=VMEM_LIMIT_BYTES),
        name="mixer",
    )(x, x, x, *p)


def _ffn_body(x_ref, w1_ref, b1_ref, w2_ref, b2_ref, g_ref, bt_ref, o_ref):
    x = x_ref[...]
    h = jnp.maximum(_dot(x.astype(BF16), w1_ref[...]) + b1_ref[...], 0.0)
    ff = _dot((h * h).astype(BF16), w2_ref[...]) + b2_ref[...]
    o_ref[...] = _ln(ALPHA * x + ff, g_ref[...], bt_ref[...])


def _ffn(x, tile, p):
    n = x.shape[0]
    return pl.pallas_call(
        _ffn_body,
        grid=(n // tile,),
        in_specs=[pl.BlockSpec((tile, D), lambda i: (i, 0)),
                  _const_spec((D, D_FF)), _const_spec((1, D_FF)), _const_spec((D_FF, D)),
                  _const_spec((1, D)), _const_spec((1, D)), _const_spec((1, D))],
        out_specs=pl.BlockSpec((tile, D), lambda i: (i, 0)),
        out_shape=jax.ShapeDtypeStruct((n, D), F32),
        compiler_params=pltpu.CompilerParams(
            dimension_semantics=("arbitrary",), vmem_limit_bytes=VMEM_LIMIT_BYTES),
        name="ffn",
    )(x, *p)


MIXER_TILE = 256
FFN_TILE = 512


def kernel(x_prompt, x_sample, ln_in_g, ln_in_b, w_in, b_in, sgu_ln_g, sgu_ln_b, sgu_w, sgu_b, sconv_w, cconv_w, cconv_b, cnorm_g, cnorm_b, w_branch, w_out, b_out, ln1_g, ln1_b, w_ff1, b_ff1, w_ff2, b_ff2, ln2_g, ln2_b):
    row = lambda a: a.reshape(1, -1)
    mixer_params, ffn_params = [], []
    for l in range(DEPTH):
        mixer_params.append((
            row(ln_in_g), row(ln_in_b),
            w_in[l].astype(BF16), row(b_in[l]),
            row(sgu_ln_g[l]), row(sgu_ln_b[l]),
            sgu_w[l].astype(BF16),
            jnp.broadcast_to(sgu_b[l][:, :, None], (HEADS, CHUNK, HEAD_DIM)),
            sconv_w[l], cconv_w[l],
            row(cconv_b[l]), row(cnorm_g[l]), row(cnorm_b[l]),
            w_branch[l].astype(BF16), w_out[l].astype(BF16), row(b_out[l]),
            row(ln1_g[l]), row(ln1_b[l])))
        ffn_params.append((
            w_ff1[l].astype(BF16), row(b_ff1[l]), w_ff2[l].astype(BF16), row(b_ff2[l]),
            row(ln2_g[l]), row(ln2_b[l])))

    def trunk(x3d):
        seq_len = x3d.shape[1]
        x = x3d.reshape(-1, D)
        for l in range(DEPTH):
            x = _mixer(x, seq_len, MIXER_TILE, l == 0, mixer_params[l])
            x = _ffn(x, FFN_TILE, ffn_params[l])
        return x.reshape(x3d.shape)

    return trunk(x_prompt), trunk(x_sample)
```

```python
import functools

import jax
import jax.numpy as jnp
from jax.experimental import pallas as pl
from jax.experimental.pallas import tpu as pltpu

D = 1024
D_FF = 4 * D
DEPTH = 2
CHUNK = 128
HEADS = 8
HEAD_DIM = D // HEADS
SHORT_W = 3
CONF_W = 31
ALPHA = float((2 * DEPTH) ** 0.25)
LN_EPS = 1e-5

LANES = 128
SLABS = D // LANES
HALO = 16
ROW_STRIDE = 4
PIECE = 256
VMEM_LIMIT_BYTES = 63 * 1024 * 1024

C_U, C_V, C_BG, C_CG, C_H, C_GA, C_GB, C_GATE = (i * D for i in range(8))

BF16 = jnp.bfloat16
F32 = jnp.float32


def _ln(x, g, b):
    mu = jnp.mean(x, axis=-1, keepdims=True)
    xc = x - mu
    var = jnp.mean(xc * xc, axis=-1, keepdims=True)
    return xc * jax.lax.rsqrt(var + LN_EPS) * g + b


def _gelu(x):
    return 0.5 * x * (1.0 + jax.lax.erf(x * (2.0 ** -0.5)))


def _sigmoid(x):
    return 1.0 / (1.0 + jnp.exp(-x))


def _dot(a, b):
    return jnp.dot(a, b, preferred_element_type=F32)


def _mixer_body(tile, tiles_per_seq, apply_ln_in,
                xp_ref, xc_ref, xn_ref, lnin_g, lnin_b, w_in, b_in, sln_g, sln_b,
                sgu_w, sgu_bb, sconv_w, cconv_w, cconv_b, cn_g, cn_b,
                w_br, w_out, b_out, ln1_g, ln1_b,
                o_ref,
                xb_scr, xres_scr, v_scr, sg_scr, g_scr, bg_scr, c_scr, yb_scr, yc_scr,
                v_scr32, u_scr, gate_scr):
    rows = tile + 2 * HALO
    nq = tile // ROW_STRIDE
    i = pl.program_id(0)
    pos = i % tiles_per_seq
    first = pos == 0
    last = pos == tiles_per_seq - 1

    def proj(x_bf16, col, q):
        cols = slice(col + q * PIECE, col + (q + 1) * PIECE)
        return _dot(x_bf16, w_in[:, cols]) + b_in[:, cols]

    xp, xc, xn = xp_ref[...], xc_ref[...], xn_ref[...]
    if apply_ln_in:
        xp = _ln(xp, lnin_g[...], lnin_b[...])
        xc = _ln(xc, lnin_g[...], lnin_b[...])
        xn = _ln(xn, lnin_g[...], lnin_b[...])
    xres_scr[...] = xc
    xb_scr[0:HALO, :] = xp.astype(BF16)
    xb_scr[HALO:HALO + tile, :] = xc.astype(BF16)
    xb_scr[HALO + tile:rows, :] = xn.astype(BF16)
    xb_c = xb_scr[HALO:HALO + tile, :]
    xb_all = xb_scr[...]
    n_pieces = D // PIECE
    slabs_per_piece = PIECE // LANES

    def store_slabs(dst, q, val):
        for jj in range(slabs_per_piece):
            j = q * slabs_per_piece + jj
            lanes = slice(jj * LANES, (jj + 1) * LANES)
            dst[j, 0:HALO, :] = jnp.where(first, 0.0, val[0:HALO, lanes])
            dst[j, HALO:HALO + tile, :] = val[HALO:HALO + tile, lanes]
            dst[j, HALO + tile:rows, :] = jnp.where(last, 0.0, val[HALO + tile:rows, lanes])

    def gather_slabs(src):
        return jnp.concatenate([src[j] for j in range(SLABS)], axis=-1)

    def conv31_unit(j, r):
        lanes = slice(j * LANES, (j + 1) * LANES)
        acc = None
        for k in range(CONF_W):
            ck = c_scr[j, pl.ds(HALO + r + k - CONF_W // 2, nq, stride=ROW_STRIDE), :]
            term = ck * cconv_w[k:k + 1, lanes]
            acc = term if acc is None else acc + term
        yc_scr[j, pl.ds(r, nq, stride=ROW_STRIDE), :] = acc + cconv_b[:, lanes]

    def conv3_unit(j, r):
        lanes = slice(j * LANES, (j + 1) * LANES)
        acc = None
        for k in range(SHORT_W):
            gk = g_scr[j, pl.ds(HALO + r + k - SHORT_W // 2, nq, stride=ROW_STRIDE), :]
            term = gk * sconv_w[k:k + 1, lanes]
            acc = term if acc is None else acc + term
        yb_scr[j, pl.ds(r, nq, stride=ROW_STRIDE), :] = (
            acc * bg_scr[j, pl.ds(r, nq, stride=ROW_STRIDE), :])

    for q in range(n_pieces):
        store_slabs(c_scr, q, proj(xb_all, C_GA, q) * _sigmoid(proj(xb_all, C_GB, q)))

    def piece_cg_h(q):
        store_slabs(g_scr, q, proj(xb_all, C_CG, q) * proj(xb_all, C_H, q))

    def piece_bg(q):
        bg = proj(xb_c, C_BG, q)
        for jj in range(slabs_per_piece):
            bg_scr[q * slabs_per_piece + jj] = bg[:, jj * LANES:(jj + 1) * LANES]

    def piece_v(q):
        v_scr32[:, q * PIECE:(q + 1) * PIECE] = _gelu(proj(xb_c, C_V, q))

    def piece_u(q):
        u_scr[:, q * PIECE:(q + 1) * PIECE] = _gelu(proj(xb_c, C_U, q))

    def piece_gate(n, q):
        gate_scr[n, :, q * PIECE:(q + 1) * PIECE] = _sigmoid(proj(xb_c, C_GATE + n * D, q))

    pieces = ([functools.partial(piece_cg_h, q) for q in range(n_pieces)]
              + [functools.partial(piece_bg, q) for q in range(n_pieces)]
              + [functools.partial(piece_v, q) for q in range(n_pieces)]
              + [functools.partial(piece_u, q) for q in range(n_pieces)]
              + [functools.partial(piece_gate, n, q) for n in range(3) for q in range(n_pieces)])
    conv31_units = [(j, r) for j in range(SLABS) for r in range(ROW_STRIDE)]
    for unit in conv31_units:
        conv31_unit(*unit)
        if pieces:
            pieces.pop(0)()
    for piece in pieces:
        piece()

    v_scr[...] = _ln(v_scr32[...], sln_g[...], sln_b[...]).astype(BF16)
    n_chunks = tile // CHUNK
    for h in range(HEADS):
        lanes = slice(h * HEAD_DIM, (h + 1) * HEAD_DIM)
        rhs = jnp.concatenate(
            [v_scr[c * CHUNK:(c + 1) * CHUNK, lanes] for c in range(n_chunks)], axis=-1)
        mixed = _dot(sgu_w[h], rhs)
        for c in range(n_chunks):
            sg_scr[c * CHUNK:(c + 1) * CHUNK, lanes] = (
                mixed[:, c * HEAD_DIM:(c + 1) * HEAD_DIM] + sgu_bb[h])
        for r in range(ROW_STRIDE):
            conv3_unit(h, r)

    y_b = gather_slabs(yb_scr)
    merged = gate_scr[1] * _dot(y_b.astype(BF16), w_br[1])
    y_a = u_scr[...] * sg_scr[...]
    merged = merged + gate_scr[0] * _dot(y_a.astype(BF16), w_br[0])
    cn = _ln(gather_slabs(yc_scr), cn_g[...], cn_b[...])
    y_c = cn * _sigmoid(cn)
    merged = merged + gate_scr[2] * _dot(y_c.astype(BF16), w_br[2])

    mix = _dot(merged.astype(BF16), w_out[...]) + b_out[...]
    o_ref[...] = _ln(ALPHA * xres_scr[...] + mix, ln1_g[...], ln1_b[...])


def _const_spec(shape):
    zeros = (0,) * len(shape)
    return pl.BlockSpec(shape, lambda i: zeros, pipeline_mode=pl.Buffered(1))


def _mixer(x, seq_len, tile, apply_ln_in, p):
    n = x.shape[0]
    rows = tile + 2 * HALO
    halo_blocks_per_tile = tile // HALO
    last_halo_block = n // HALO - 1
    body = functools.partial(_mixer_body, tile, seq_len // tile, apply_ln_in)
    row_spec = lambda: _const_spec((1, D))
    in_specs = [
        pl.BlockSpec((HALO, D), lambda i: (jnp.maximum(i * halo_blocks_per_tile - 1, 0), 0)),
        pl.BlockSpec((tile, D), lambda i: (i, 0)),
        pl.BlockSpec((HALO, D),
                     lambda i: (jnp.minimum((i + 1) * halo_blocks_per_tile, last_halo_block), 0)),
        row_spec(), row_spec(),
        _const_spec((D, 10 * D)), _const_spec((1, 10 * D)),
        row_spec(), row_spec(),
        _const_spec((HEADS, CHUNK, CHUNK)), _const_spec((HEADS, CHUNK, HEAD_DIM)),
        _const_spec((SHORT_W, D)), _const_spec((CONF_W, D)),
        row_spec(), row_spec(), row_spec(),
        _const_spec((3, D, D)), _const_spec((D, D)), row_spec(),
        row_spec(), row_spec(),
    ]
    scratch = [
        pltpu.VMEM((rows, D), BF16),
        pltpu.VMEM((tile, D), F32),
        pltpu.VMEM((tile, D), BF16),
        pltpu.VMEM((tile, D), F32),
        pltpu.VMEM((SLABS, rows, LANES), F32),
        pltpu.VMEM((SLABS, tile, LANES), F32),
        pltpu.VMEM((SLABS, rows, LANES), F32),
        pltpu.VMEM((SLABS, tile, LANES), F32),
        pltpu.VMEM((SLABS, tile, LANES), F32),
        pltpu.VMEM((tile, D), F32),
        pltpu.VMEM((tile, D), F32),
        pltpu.VMEM((3, tile, D), F32),
    ]
    return pl.pallas_call(
        body,
        grid=(n // tile,),
        in_specs=in_specs,
        out_specs=pl.BlockSpec((tile, D), lambda i: (i, 0)),
        out_shape=jax.ShapeDtypeStruct((n, D), F32),
        scratch_shapes=scratch,
        compiler_params=pltpu.CompilerParams(
            dimension_semantics=("arbitrary",), vmem_limit_bytes=VMEM_LIMIT_BYTES),
        name="mixer",
    )(x, x, x, *p)


def _ffn_body(x_ref, w1_ref, b1_ref, w2_ref, b2_ref, g_ref, bt_ref, o_ref):
    x = x_ref[...]
    h = jnp.maximum(_dot(x.astype(BF16), w1_ref[...]) + b1_ref[...], 0.0)
    ff = _dot((h * h).astype(BF16), w2_ref[...]) + b2_ref[...]
    o_ref[...] = _ln(ALPHA * x + ff, g_ref[...], bt_ref[...])


def _ffn(x, tile, p):
    n = x.shape[0]
    return pl.pallas_call(
        _ffn_body,
        grid=(n // tile,),
        in_specs=[pl.BlockSpec((tile, D), lambda i: (i, 0)),
                  _const_spec((D, D_FF)), _const_spec((1, D_FF)), _const_spec((D_FF, D)),
                  _const_spec((1, D)), _const_spec((1, D)), _const_spec((1, D))],
        out_specs=pl.BlockSpec((tile, D), lambda i: (i, 0)),
        out_shape=jax.ShapeDtypeStruct((n, D), F32),
        compiler_params=pltpu.CompilerParams(
            dimension_semantics=("arbitrary",), vmem_limit_bytes=VMEM_LIMIT_BYTES),
        name="ffn",
    )(x, *p)


MIXER_TILE = 512
FFN_TILE = 1024


def kernel(x_prompt, x_sample, ln_in_g, ln_in_b, w_in, b_in, sgu_ln_g, sgu_ln_b, sgu_w, sgu_b, sconv_w, cconv_w, cconv_b, cnorm_g, cnorm_b, w_branch, w_out, b_out, ln1_g, ln1_b, w_ff1, b_ff1, w_ff2, b_ff2, ln2_g, ln2_b):
    row = lambda a: a.reshape(1, -1)
    mixer_params, ffn_params = [], []
    for l in range(DEPTH):
        mixer_params.append((
            row(ln_in_g), row(ln_in_b),
            w_in[l].astype(BF16), row(b_in[l]),
            row(sgu_ln_g[l]), row(sgu_ln_b[l]),
            sgu_w[l].astype(BF16),
            jnp.broadcast_to(sgu_b[l][:, :, None], (HEADS, CHUNK, HEAD_DIM)),
            sconv_w[l], cconv_w[l],
            row(cconv_b[l]), row(cnorm_g[l]), row(cnorm_b[l]),
            w_branch[l].astype(BF16), w_out[l].astype(BF16), row(b_out[l]),
            row(ln1_g[l]), row(ln1_b[l])))
        ffn_params.append((
            w_ff1[l].astype(BF16), row(b_ff1[l]), w_ff2[l].astype(BF16), row(b_ff2[l]),
            row(ln2_g[l]), row(ln2_b[l])))

    def trunk(x3d):
        seq_len = x3d.shape[1]
        x = x3d.reshape(-1, D)
        for l in range(DEPTH):
            x = _mixer(x, seq_len, MIXER_TILE, l == 0, mixer_params[l])
            x = _ffn(x, FFN_TILE, ffn_params[l])
        return x.reshape(x3d.shape)

    return trunk(x_prompt), trunk(x_sample)
```

```python
import functools

import jax
import jax.numpy as jnp
from jax.experimental import pallas as pl
from jax.experimental.pallas import tpu as pltpu

D = 1024
D_FF = 4 * D
DEPTH = 2
CHUNK = 128
HEADS = 8
HEAD_DIM = D // HEADS
SHORT_W = 3
CONF_W = 31
ALPHA = float((2 * DEPTH) ** 0.25)
LN_EPS = 1e-5

LANES = 128
SLABS = D // LANES
HALO = 16
ROW_STRIDE = 2
PIECE = 256
VMEM_LIMIT_BYTES = 63 * 1024 * 1024

C_U, C_V, C_BG, C_CG, C_H, C_GA, C_GB, C_GATE = (i * D for i in range(8))

BF16 = jnp.bfloat16
F32 = jnp.float32


def _ln(x, g, b):
    mu = jnp.mean(x, axis=-1, keepdims=True)
    xc = x - mu
    var = jnp.mean(xc * xc, axis=-1, keepdims=True)
    return xc * jax.lax.rsqrt(var + LN_EPS) * g + b


def _gelu(x):
    return 0.5 * x * (1.0 + jax.lax.erf(x * (2.0 ** -0.5)))


def _sigmoid(x):
    return 1.0 / (1.0 + jnp.exp(-x))


def _dot(a, b):
    return jnp.dot(a, b, preferred_element_type=F32)


def _mixer_body(tile, tiles_per_seq, apply_ln_in,
                xp_ref, xc_ref, xn_ref, lnin_g, lnin_b, w_in, b_in, sln_g, sln_b,
                sgu_w, sgu_bb, sconv_w, cconv_w, cconv_b, cn_g, cn_b,
                w_br, w_out, b_out, ln1_g, ln1_b,
                o_ref,
                xb_scr, xres_scr, v_scr, sg_scr, g_scr, bg_scr, c_scr, yb_scr, yc_scr,
                v_scr32, u_scr, gate_scr):
    rows = tile + 2 * HALO
    nq = tile // ROW_STRIDE
    i = pl.program_id(0)
    pos = i % tiles_per_seq
    first = pos == 0
    last = pos == tiles_per_seq - 1

    def proj(x_bf16, col, q):
        cols = slice(col + q * PIECE, col + (q + 1) * PIECE)
        return _dot(x_bf16, w_in[:, cols]) + b_in[:, cols]

    xp, xc, xn = xp_ref[...], xc_ref[...], xn_ref[...]
    if apply_ln_in:
        xp = _ln(xp, lnin_g[...], lnin_b[...])
        xc = _ln(xc, lnin_g[...], lnin_b[...])
        xn = _ln(xn, lnin_g[...], lnin_b[...])
    xres_scr[...] = xc
    xb_scr[0:HALO, :] = xp.astype(BF16)
    xb_scr[HALO:HALO + tile, :] = xc.astype(BF16)
    xb_scr[HALO + tile:rows, :] = xn.astype(BF16)
    xb_c = xb_scr[HALO:HALO + tile, :]
    xb_all = xb_scr[...]
    n_pieces = D // PIECE
    slabs_per_piece = PIECE // LANES

    def store_slabs(dst, q, val):
        for jj in range(slabs_per_piece):
            j = q * slabs_per_piece + jj
            lanes = slice(jj * LANES, (jj + 1) * LANES)
            dst[j, 0:HALO, :] = jnp.where(first, 0.0, val[0:HALO, lanes])
            dst[j, HALO:HALO + tile, :] = val[HALO:HALO + tile, lanes]
            dst[j, HALO + tile:rows, :] = jnp.where(last, 0.0, val[HALO + tile:rows, lanes])

    def gather_slabs(src):
        return jnp.concatenate([src[j] for j in range(SLABS)], axis=-1)

    def conv31_unit(j, r):
        lanes = slice(j * LANES, (j + 1) * LANES)
        acc = None
        for k in range(CONF_W):
            ck = c_scr[j, pl.ds(HALO + r + k - CONF_W // 2, nq, stride=ROW_STRIDE), :]
            term = ck * cconv_w[k:k + 1, lanes]
            acc = term if acc is None else acc + term
        yc_scr[j, pl.ds(r, nq, stride=ROW_STRIDE), :] = acc + cconv_b[:, lanes]

    def conv3_unit(j, r):
        lanes = slice(j * LANES, (j + 1) * LANES)
        acc = None
        for k in range(SHORT_W):
            gk = g_scr[j, pl.ds(HALO + r + k - SHORT_W // 2, nq, stride=ROW_STRIDE), :]
            term = gk * sconv_w[k:k + 1, lanes]
            acc = term if acc is None else acc + term
        yb_scr[j, pl.ds(r, nq, stride=ROW_STRIDE), :] = (
            acc * bg_scr[j, pl.ds(r, nq, stride=ROW_STRIDE), :])

    for q in range(n_pieces):
        store_slabs(c_scr, q, proj(xb_all, C_GA, q) * _sigmoid(proj(xb_all, C_GB, q)))

    def piece_cg_h(q):
        store_slabs(g_scr, q, proj(xb_all, C_CG, q) * proj(xb_all, C_H, q))

    def piece_bg(q):
        bg = proj(xb_c, C_BG, q)
        for jj in range(slabs_per_piece):
            bg_scr[q * slabs_per_piece + jj] = bg[:, jj * LANES:(jj + 1) * LANES]

    def piece_v(q):
        v_scr32[:, q * PIECE:(q + 1) * PIECE] = _gelu(proj(xb_c, C_V, q))

    def piece_u(q):
        u_scr[:, q * PIECE:(q + 1) * PIECE] = _gelu(proj(xb_c, C_U, q))

    def piece_gate(n, q):
        gate_scr[n, :, q * PIECE:(q + 1) * PIECE] = _sigmoid(proj(xb_c, C_GATE + n * D, q))

    pieces = ([functools.partial(piece_cg_h, q) for q in range(n_pieces)]
              + [functools.partial(piece_bg, q) for q in range(n_pieces)]
              + [functools.partial(piece_v, q) for q in range(n_pieces)]
              + [functools.partial(piece_u, q) for q in range(n_pieces)]
              + [functools.partial(piece_gate, n, q) for n in range(3) for q in range(n_pieces)])
    conv31_units = [(j, r) for j in range(SLABS) for r in range(ROW_STRIDE)]
    n_total = len(pieces)
    for idx, unit in enumerate(conv31_units):
        conv31_unit(*unit)
        while len(pieces) > n_total - (idx + 1) * n_total // len(conv31_units):
            pieces.pop(0)()
    assert not pieces

    v_scr[...] = _ln(v_scr32[...], sln_g[...], sln_b[...]).astype(BF16)
    n_chunks = tile // CHUNK
    for h in range(HEADS):
        lanes = slice(h * HEAD_DIM, (h + 1) * HEAD_DIM)
        rhs = jnp.concatenate(
            [v_scr[c * CHUNK:(c + 1) * CHUNK, lanes] for c in range(n_chunks)], axis=-1)
        mixed = _dot(sgu_w[h], rhs)
        for c in range(n_chunks):
            sg_scr[c * CHUNK:(c + 1) * CHUNK, lanes] = (
                mixed[:, c * HEAD_DIM:(c + 1) * HEAD_DIM] + sgu_bb[h])
        for r in range(ROW_STRIDE):
            conv3_unit(h, r)

    y_b = gather_slabs(yb_scr)
    merged = gate_scr[1] * _dot(y_b.astype(BF16), w_br[1])
    y_a = u_scr[...] * sg_scr[...]
    merged = merged + gate_scr[0] * _dot(y_a.astype(BF16), w_br[0])
    cn = _ln(gather_slabs(yc_scr), cn_g[...], cn_b[...])
    y_c = cn * _sigmoid(cn)
    merged = merged + gate_scr[2] * _dot(y_c.astype(BF16), w_br[2])

    mix = _dot(merged.astype(BF16), w_out[...]) + b_out[...]
    o_ref[...] = _ln(ALPHA * xres_scr[...] + mix, ln1_g[...], ln1_b[...])


def _const_spec(shape):
    zeros = (0,) * len(shape)
    return pl.BlockSpec(shape, lambda i: zeros, pipeline_mode=pl.Buffered(1))


def _mixer(x, seq_len, tile, apply_ln_in, p):
    n = x.shape[0]
    rows = tile + 2 * HALO
    halo_blocks_per_tile = tile // HALO
    last_halo_block = n // HALO - 1
    body = functools.partial(_mixer_body, tile, seq_len // tile, apply_ln_in)
    row_spec = lambda: _const_spec((1, D))
    in_specs = [
        pl.BlockSpec((HALO, D), lambda i: (jnp.maximum(i * halo_blocks_per_tile - 1, 0), 0)),
        pl.BlockSpec((tile, D), lambda i: (i, 0)),
        pl.BlockSpec((HALO, D),
                     lambda i: (jnp.minimum((i + 1) * halo_blocks_per_tile, last_halo_block), 0)),
        row_spec(), row_spec(),
        _const_spec((D, 10 * D)), _const_spec((1, 10 * D)),
        row_spec(), row_spec(),
        _const_spec((HEADS, CHUNK, CHUNK)), _const_spec((HEADS, CHUNK, HEAD_DIM)),
        _const_spec((SHORT_W, D)), _const_spec((CONF_W, D)),
        row_spec(), row_spec(), row_spec(),
        _const_spec((3, D, D)), _const_spec((D, D)), row_spec(),
        row_spec(), row_spec(),
    ]
    scratch = [
        pltpu.VMEM((rows, D), BF16),
        pltpu.VMEM((tile, D), F32),
        pltpu.VMEM((tile, D), BF16),
        pltpu.VMEM((tile, D), F32),
        pltpu.VMEM((SLABS, rows, LANES), F32),
        pltpu.VMEM((SLABS, tile, LANES), F32),
        pltpu.VMEM((SLABS, rows, LANES), F32),
        pltpu.VMEM((SLABS, tile, LANES), F32),
        pltpu.VMEM((SLABS, tile, LANES), F32),
        pltpu.VMEM((tile, D), F32),
        pltpu.VMEM((tile, D), F32),
        pltpu.VMEM((3, tile, D), F32),
    ]
    return pl.pallas_call(
        body,
        grid=(n // tile,),
        in_specs=in_specs,
        out_specs=pl.BlockSpec((tile, D), lambda i: (i, 0)),
        out_shape=jax.ShapeDtypeStruct((n, D), F32),
        scratch_shapes=scratch,
        compiler_params=pltpu.CompilerParams(
            dimension_semantics=("arbitrary",), vmem_limit_bytes=VMEM_LIMIT_BYTES),
        name="mixer",
    )(x, x, x, *p)


def _ffn_body(x_ref, w1_ref, b1_ref, w2_ref, b2_ref, g_ref, bt_ref, o_ref):
    x = x_ref[...]
    h = jnp.maximum(_dot(x.astype(BF16), w1_ref[...]) + b1_ref[...], 0.0)
    ff = _dot((h * h).astype(BF16), w2_ref[...]) + b2_ref[...]
    o_ref[...] = _ln(ALPHA * x + ff, g_ref[...], bt_ref[...])


def _ffn(x, tile, p):
    n = x.shape[0]
    return pl.pallas_call(
        _ffn_body,
        grid=(n // tile,),
        in_specs=[pl.BlockSpec((tile, D), lambda i: (i, 0)),
                  _const_spec((D, D_FF)), _const_spec((1, D_FF)), _const_spec((D_FF, D)),
                  _const_spec((1, D)), _const_spec((1, D)), _const_spec((1, D))],
        out_specs=pl.BlockSpec((tile, D), lambda i: (i, 0)),
        out_shape=jax.ShapeDtypeStruct((n, D), F32),
        compiler_params=pltpu.CompilerParams(
            dimension_semantics=("arbitrary",), vmem_limit_bytes=VMEM_LIMIT_BYTES),
        name="ffn",
    )(x, *p)


MIXER_TILE = 512
FFN_TILE = 1024


def kernel(x_prompt, x_sample, ln_in_g, ln_in_b, w_in, b_in, sgu_ln_g, sgu_ln_b, sgu_w, sgu_b, sconv_w, cconv_w, cconv_b, cnorm_g, cnorm_b, w_branch, w_out, b_out, ln1_g, ln1_b, w_ff1, b_ff1, w_ff2, b_ff2, ln2_g, ln2_b):
    row = lambda a: a.reshape(1, -1)
    mixer_params, ffn_params = [], []
    for l in range(DEPTH):
        mixer_params.append((
            row(ln_in_g), row(ln_in_b),
            w_in[l].astype(BF16), row(b_in[l]),
            row(sgu_ln_g[l]), row(sgu_ln_b[l]),
            sgu_w[l].astype(BF16),
            jnp.broadcast_to(sgu_b[l][:, :, None], (HEADS, CHUNK, HEAD_DIM)),
            sconv_w[l], cconv_w[l],
            row(cconv_b[l]), row(cnorm_g[l]), row(cnorm_b[l]),
            w_branch[l].astype(BF16), w_out[l].astype(BF16), row(b_out[l]),
            row(ln1_g[l]), row(ln1_b[l])))
        ffn_params.append((
            w_ff1[l].astype(BF16), row(b_ff1[l]), w_ff2[l].astype(BF16), row(b_ff2[l]),
            row(ln2_g[l]), row(ln2_b[l])))

    def trunk(x3d):
        seq_len = x3d.shape[1]
        x = x3d.reshape(-1, D)
        for l in range(DEPTH):
            x = _mixer(x, seq_len, MIXER_TILE, l == 0, mixer_params[l])
            x = _ffn(x, FFN_TILE, ffn_params[l])
        return x.reshape(x3d.shape)

    return trunk(x_prompt), trunk(x_sample)
```

```python
import functools

import jax
import jax.numpy as jnp
from jax.experimental import pallas as pl
from jax.experimental.pallas import tpu as pltpu

D = 1024
D_FF = 4 * D
DEPTH = 2
CHUNK = 128
HEADS = 8
HEAD_DIM = D // HEADS
SHORT_W = 3
CONF_W = 31
ALPHA = float((2 * DEPTH) ** 0.25)
LN_EPS = 1e-5

LANES = 128
SLABS = D // LANES
HALO = 16
ROW_STRIDE = 2
PIECE = 256
VMEM_LIMIT_BYTES = 63 * 1024 * 1024

C_U, C_V, C_BG, C_CG, C_H, C_GA, C_GB, C_GATE = (i * D for i in range(8))

BF16 = jnp.bfloat16
F32 = jnp.float32


def _ln(x, g, b):
    mu = jnp.mean(x, axis=-1, keepdims=True)
    xc = x - mu
    var = jnp.mean(xc * xc, axis=-1, keepdims=True)
    return xc * jax.lax.rsqrt(var + LN_EPS) * g + b


def _gelu(x):
    return 0.5 * x * (1.0 + jax.lax.erf(x * (2.0 ** -0.5)))


def _sigmoid(x):
    return 1.0 / (1.0 + jnp.exp(-x))


def _dot(a, b):
    return jnp.dot(a, b, preferred_element_type=F32)


def _mixer_body(tile, tiles_per_seq, apply_ln_in,
                xc_ref, xn_ref, lnin_g, lnin_b, w_in, b_in, sln_g, sln_b,
                sgu_w, sgu_bb, sconv_w, cconv_w, cconv_b, cn_g, cn_b,
                w_br, w_out, b_out, ln1_g, ln1_b,
                o_ref,
                xb_scr, xres_scr, v_scr, sg_scr, g_scr, bg_scr, c_scr, yb_scr, yc_scr,
                v_scr32, u_scr, gate_scr):
    rows = tile + 2 * HALO
    nq = tile // ROW_STRIDE
    i = pl.program_id(0)
    pos = i % tiles_per_seq
    first = pos == 0
    last = pos == tiles_per_seq - 1

    @pl.when(i == 0)
    def _():
        c_scr[...] = jnp.zeros(c_scr.shape, c_scr.dtype)
        g_scr[...] = jnp.zeros(g_scr.shape, g_scr.dtype)

    def proj(x_bf16, col, q):
        cols = slice(col + q * PIECE, col + (q + 1) * PIECE)
        return _dot(x_bf16, w_in[:, cols]) + b_in[:, cols]

    xc, xn = xc_ref[...], xn_ref[...]
    if apply_ln_in:
        xc = _ln(xc, lnin_g[...], lnin_b[...])
        xn = _ln(xn, lnin_g[...], lnin_b[...])
    xres_scr[...] = xc
    xb_scr[HALO:HALO + tile, :] = xc.astype(BF16)
    xb_scr[HALO + tile:rows, :] = xn.astype(BF16)
    xb_c = xb_scr[HALO:HALO + tile, :]
    xb_all = xb_scr[HALO:rows, :]
    n_pieces = D // PIECE
    slabs_per_piece = PIECE // LANES

    def store_slabs(dst, q, val):
        for jj in range(slabs_per_piece):
            j = q * slabs_per_piece + jj
            lanes = slice(jj * LANES, (jj + 1) * LANES)
            dst[j, 0:HALO, :] = jnp.where(first, 0.0, dst[j, tile:tile + HALO, :])
            dst[j, HALO:HALO + tile, :] = val[0:tile, lanes]
            dst[j, HALO + tile:rows, :] = jnp.where(last, 0.0, val[tile:tile + HALO, lanes])

    def gather_slabs(src):
        return jnp.concatenate([src[j] for j in range(SLABS)], axis=-1)

    def conv31_unit(j, r):
        lanes = slice(j * LANES, (j + 1) * LANES)
        acc = None
        for k in range(CONF_W):
            ck = c_scr[j, pl.ds(HALO + r + k - CONF_W // 2, nq, stride=ROW_STRIDE), :]
            term = ck * cconv_w[k:k + 1, lanes]
            acc = term if acc is None else acc + term
        yc_scr[j, pl.ds(r, nq, stride=ROW_STRIDE), :] = acc + cconv_b[:, lanes]

    def conv3_unit(j, r):
        lanes = slice(j * LANES, (j + 1) * LANES)
        acc = None
        for k in range(SHORT_W):
            gk = g_scr[j, pl.ds(HALO + r + k - SHORT_W // 2, nq, stride=ROW_STRIDE), :]
            term = gk * sconv_w[k:k + 1, lanes]
            acc = term if acc is None else acc + term
        yb_scr[j, pl.ds(r, nq, stride=ROW_STRIDE), :] = (
            acc * bg_scr[j, pl.ds(r, nq, stride=ROW_STRIDE), :])

    for q in range(n_pieces):
        store_slabs(c_scr, q, proj(xb_all, C_GA, q) * _sigmoid(proj(xb_all, C_GB, q)))

    def piece_cg_h(q):
        store_slabs(g_scr, q, proj(xb_all, C_CG, q) * proj(xb_all, C_H, q))

    def piece_bg(q):
        bg = proj(xb_c, C_BG, q)
        for jj in range(slabs_per_piece):
            bg_scr[q * slabs_per_piece + jj] = bg[:, jj * LANES:(jj + 1) * LANES]

    def piece_v(q):
        v_scr32[:, q * PIECE:(q + 1) * PIECE] = _gelu(proj(xb_c, C_V, q))

    def piece_u(q):
        u_scr[:, q * PIECE:(q + 1) * PIECE] = _gelu(proj(xb_c, C_U, q))

    def piece_gate(n, q):
        gate_scr[n, :, q * PIECE:(q + 1) * PIECE] = _sigmoid(proj(xb_c, C_GATE + n * D, q))

    pieces = ([functools.partial(piece_cg_h, q) for q in range(n_pieces)]
              + [functools.partial(piece_bg, q) for q in range(n_pieces)]
              + [functools.partial(piece_v, q) for q in range(n_pieces)]
              + [functools.partial(piece_u, q) for q in range(n_pieces)]
              + [functools.partial(piece_gate, n, q) for n in range(3) for q in range(n_pieces)])
    conv31_units = [(j, r) for j in range(SLABS) for r in range(ROW_STRIDE)]
    n_total = len(pieces)
    for idx, unit in enumerate(conv31_units):
        conv31_unit(*unit)
        while len(pieces) > n_total - (idx + 1) * n_total // len(conv31_units):
            pieces.pop(0)()
    assert not pieces

    v_scr[...] = _ln(v_scr32[...], sln_g[...], sln_b[...]).astype(BF16)
    n_chunks = tile // CHUNK
    for h in range(HEADS):
        lanes = slice(h * HEAD_DIM, (h + 1) * HEAD_DIM)
        rhs = jnp.concatenate(
            [v_scr[c * CHUNK:(c + 1) * CHUNK, lanes] for c in range(n_chunks)], axis=-1)
        mixed = _dot(sgu_w[h], rhs)
        for c in range(n_chunks):
            sg_scr[c * CHUNK:(c + 1) * CHUNK, lanes] = (
                mixed[:, c * HEAD_DIM:(c + 1) * HEAD_DIM] + sgu_bb[h])
        for r in range(ROW_STRIDE):
            conv3_unit(h, r)

    y_b = gather_slabs(yb_scr)
    merged = gate_scr[1] * _dot(y_b.astype(BF16), w_br[1])
    y_a = u_scr[...] * sg_scr[...]
    merged = merged + gate_scr[0] * _dot(y_a.astype(BF16), w_br[0])
    cn = _ln(gather_slabs(yc_scr), cn_g[...], cn_b[...])
    y_c = cn * _sigmoid(cn)
    merged = merged + gate_scr[2] * _dot(y_c.astype(BF16), w_br[2])

    mix = _dot(merged.astype(BF16), w_out[...]) + b_out[...]
    o_ref[...] = _ln(ALPHA * xres_scr[...] + mix, ln1_g[...], ln1_b[...])


def _const_spec(shape):
    zeros = (0,) * len(shape)
    return pl.BlockSpec(shape, lambda i: zeros, pipeline_mode=pl.Buffered(1))


def _mixer(x, seq_len, tile, apply_ln_in, p):
    n = x.shape[0]
    rows = tile + 2 * HALO
    halo_blocks_per_tile = tile // HALO
    last_halo_block = n // HALO - 1
    body = functools.partial(_mixer_body, tile, seq_len // tile, apply_ln_in)
    row_spec = lambda: _const_spec((1, D))
    in_specs = [
        pl.BlockSpec((tile, D), lambda i: (i, 0)),
        pl.BlockSpec((HALO, D),
                     lambda i: (jnp.minimum((i + 1) * halo_blocks_per_tile, last_halo_block), 0)),
        row_spec(), row_spec(),
        _const_spec((D, 10 * D)), _const_spec((1, 10 * D)),
        row_spec(), row_spec(),
        _const_spec((HEADS, CHUNK, CHUNK)), _const_spec((HEADS, CHUNK, HEAD_DIM)),
        _const_spec((SHORT_W, D)), _const_spec((CONF_W, D)),
        row_spec(), row_spec(), row_spec(),
        _const_spec((3, D, D)), _const_spec((D, D)), row_spec(),
        row_spec(), row_spec(),
    ]
    scratch = [
        pltpu.VMEM((rows, D), BF16),
        pltpu.VMEM((tile, D), F32),
        pltpu.VMEM((tile, D), BF16),
        pltpu.VMEM((tile, D), F32),
        pltpu.VMEM((SLABS, rows, LANES), F32),
        pltpu.VMEM((SLABS, tile, LANES), F32),
        pltpu.VMEM((SLABS, rows, LANES), F32),
        pltpu.VMEM((SLABS, tile, LANES), F32),
        pltpu.VMEM((SLABS, tile, LANES), F32),
        pltpu.VMEM((tile, D), F32),
        pltpu.VMEM((tile, D), F32),
        pltpu.VMEM((3, tile, D), F32),
    ]
    return pl.pallas_call(
        body,
        grid=(n // tile,),
        in_specs=in_specs,
        out_specs=pl.BlockSpec((tile, D), lambda i: (i, 0)),
        out_shape=jax.ShapeDtypeStruct((n, D), F32),
        scratch_shapes=scratch,
        compiler_params=pltpu.CompilerParams(
            dimension_semantics=("arbitrary",), vmem_limit_bytes=VMEM_LIMIT_BYTES),
        name="mixer",
    )(x, x, *p)


def _ffn_body(x_ref, w1_ref, b1_ref, w2_ref, b2_ref, g_ref, bt_ref, o_ref):
    x = x_ref[...]
    h = jnp.maximum(_dot(x.astype(BF16), w1_ref[...]) + b1_ref[...], 0.0)
    ff = _dot((h * h).astype(BF16), w2_ref[...]) + b2_ref[...]
    o_ref[...] = _ln(ALPHA * x + ff, g_ref[...], bt_ref[...])


def _ffn(x, tile, p):
    n = x.shape[0]
    return pl.pallas_call(
        _ffn_body,
        grid=(n // tile,),
        in_specs=[pl.BlockSpec((tile, D), lambda i: (i, 0)),
                  _const_spec((D, D_FF)), _const_spec((1, D_FF)), _const_spec((D_FF, D)),
                  _const_spec((1, D)), _const_spec((1, D)), _const_spec((1, D))],
        out_specs=pl.BlockSpec((tile, D), lambda i: (i, 0)),
        out_shape=jax.ShapeDtypeStruct((n, D), F32),
        compiler_params=pltpu.CompilerParams(
            dimension_semantics=("arbitrary",), vmem_limit_bytes=VMEM_LIMIT_BYTES),
        name="ffn",
    )(x, *p)


MIXER_TILE = 512
FFN_TILE = 1024


def kernel(x_prompt, x_sample, ln_in_g, ln_in_b, w_in, b_in, sgu_ln_g, sgu_ln_b, sgu_w, sgu_b, sconv_w, cconv_w, cconv_b, cnorm_g, cnorm_b, w_branch, w_out, b_out, ln1_g, ln1_b, w_ff1, b_ff1, w_ff2, b_ff2, ln2_g, ln2_b):
    row = lambda a: a.reshape(1, -1)
    mixer_params, ffn_params = [], []
    for l in range(DEPTH):
        mixer_params.append((
            row(ln_in_g), row(ln_in_b),
            w_in[l].astype(BF16), row(b_in[l]),
            row(sgu_ln_g[l]), row(sgu_ln_b[l]),
            sgu_w[l].astype(BF16),
            jnp.broadcast_to(sgu_b[l][:, :, None], (HEADS, CHUNK, HEAD_DIM)),
            sconv_w[l], cconv_w[l],
            row(cconv_b[l]), row(cnorm_g[l]), row(cnorm_b[l]),
            w_branch[l].astype(BF16), w_out[l].astype(BF16), row(b_out[l]),
            row(ln1_g[l]), row(ln1_b[l])))
        ffn_params.append((
            w_ff1[l].astype(BF16), row(b_ff1[l]), w_ff2[l].astype(BF16), row(b_ff2[l]),
            row(ln2_g[l]), row(ln2_b[l])))

    def trunk(x3d):
        seq_len = x3d.shape[1]
        x = x3d.reshape(-1, D)
        for l in range(DEPTH):
            x = _mixer(x, seq_len, MIXER_TILE, l == 0, mixer_params[l])
            x = _ffn(x, FFN_TILE, ffn_params[l])
        return x.reshape(x3d.shape)

    return trunk(x_prompt), trunk(x_sample)
```
